```python
import jax, jax.numpy as jnp
from jax import lax
import numpy as np

D_MODEL = 2048
BATCH = 8
SEQ = 2048
DEPTH = 1
DEC_BATCH = 2
DEC_SEQ = 16384
PAST_LEN = 128

D_CONV = D_MODEL // 2
D_SHORT = D_MODEL - D_CONV
CONV_WIDTH = 31
SHORT_WIDTH = 3
D_IN = 2 * D_CONV + 3 * D_SHORT
N_KEYS = 128
N_EXPERTS = N_KEYS * N_KEYS
PEER_HEADS = 8
D_QUERY = 256
D_HALF = D_QUERY // 2
TOPK = 16
PEER_CHUNK = 128
RMS_EPS = 1e-6
LN_EPS = 1e-5

kernel_name = "hybrid_conformer_shortconv_peer_adaln_encoder"


def rmsnorm(x, g):
    x32 = x.astype(jnp.float32)
    y = x32 * lax.rsqrt(jnp.mean(x32 * x32, axis=-1, keepdims=True) + RMS_EPS)
    return (y * g.astype(jnp.float32)).astype(x.dtype)


def layernorm(x, g, b):
    x32 = x.astype(jnp.float32)
    mu = jnp.mean(x32, axis=-1, keepdims=True)
    xc = x32 - mu
    var = jnp.mean(xc * xc, axis=-1, keepdims=True)
    y = xc * lax.rsqrt(var + LN_EPS) * g.astype(jnp.float32) + b.astype(jnp.float32)
    return y.astype(x.dtype)


def depthwise_conv(x, w):
    c = x.shape[-1]
    pad = (w.shape[0] - 1) // 2
    return lax.conv_general_dilated(
        x, w[:, None, :].astype(x.dtype), window_strides=(1,), padding=[(pad, pad)],
        dimension_numbers=('NWC', 'WIO', 'NWC'), feature_group_count=c)


def token_mixer(h, w_in, conv_w, conv_b, ln_g, ln_b, short_w, w_out):
    z = h @ w_in
    a, a_gate, b_gate, c_gate, xs = jnp.split(
        z, [D_CONV, 2 * D_CONV, 2 * D_CONV + D_SHORT, 2 * D_CONV + 2 * D_SHORT], axis=-1)
    u = a * jax.nn.sigmoid(a_gate)
    u = depthwise_conv(u, conv_w) + conv_b
    u = jax.nn.silu(layernorm(u, ln_g, ln_b))
    s = b_gate * depthwise_conv(c_gate * xs, short_w)
    return jnp.concatenate([u, s], axis=-1) @ w_out


def peer(h, w_query, sub_keys, expert_u, expert_v):
    bsz, seq, d = h.shape
    blocks = h.reshape(-1, PEER_CHUNK, d)

    def block_fn(xc):
        t = xc.shape[0]
        q = (xc @ w_query).reshape(t, PEER_HEADS, 2, D_HALF)
        s = jnp.einsum('thpd,hpkd->thpk', q, sub_keys).astype(jnp.float32)
        sv, si = lax.top_k(s, TOPK)
        cand = (sv[:, :, 0, :, None] + sv[:, :, 1, None, :]).reshape(t, PEER_HEADS, TOPK * TOPK)
        cidx = (si[:, :, 0, :, None] * N_KEYS + si[:, :, 1, None, :]).reshape(t, PEER_HEADS, TOPK * TOPK)
        top_s, pos = lax.top_k(cand, TOPK)
        eidx = jnp.take_along_axis(cidx, pos, axis=-1).reshape(t, PEER_HEADS * TOPK)
        gate = jax.nn.softmax(top_s, axis=-1).reshape(t, PEER_HEADS * TOPK)
        u = expert_u[eidx]
        act = jax.nn.gelu(jnp.einsum('ted,td->te', u, xc).astype(jnp.float32), approximate=False)
        coef = (gate * act).astype(xc.dtype)
        return jnp.einsum('te,ted->td', coef, expert_v[eidx])

    return lax.map(block_fn, blocks).reshape(bsz, seq, d)


def trunk(x, c, w_ada, b_ada, g_norm1, w_in, conv_w, conv_b, ln_g, ln_b, short_w, w_out,
          g_norm2, w_query, sub_keys, expert_u, expert_v, g_final):
    for l in range(DEPTH):
        mod = (jax.nn.silu(c) @ w_ada[l] + b_ada[l])[:, None, :]
        sh1, sc1, gt1, sh2, sc2, gt2 = jnp.split(mod, 6, axis=-1)
        h = rmsnorm(x, g_norm1[l]) * (1 + sc1) + sh1
        x = x + gt1 * token_mixer(h, w_in[l], conv_w[l], conv_b[l], ln_g[l], ln_b[l], short_w[l], w_out[l])
        h = rmsnorm(x, g_norm2[l]) * (1 + sc2) + sh2
        x = x + gt2 * peer(h, w_query[l], sub_keys[l], expert_u[l], expert_v[l])
    return rmsnorm(x, g_final)


def setup_inputs(seed: int = 0) -> dict:
    key = jax.random.key(seed)
    ks = jax.random.split(key, 24)
    f = jnp.float32
    n = lambda k, shape, s: jax.random.normal(k, shape, f) * s
    d = D_MODEL
    return {
        "x_prompt": n(ks[0], (BATCH, SEQ, d), 1.0),
        "x_sample": n(ks[1], (DEC_BATCH, DEC_SEQ, d), 1.0),
        "c_prompt": n(ks[2], (BATCH, d), 1.0),
        "c_sample": n(ks[3], (DEC_BATCH, d), 1.0),
        "w_ada": n(ks[4], (DEPTH, d, 6 * d), 0.5 * d ** -0.5),
        "b_ada": n(ks[5], (DEPTH, 6 * d), 0.01),
        "g_norm1": 1.0 + n(ks[6], (DEPTH, d), 0.02),
        "w_in": n(ks[7], (DEPTH, d, D_IN), d ** -0.5),
        "conv_w": n(ks[8], (DEPTH, CONV_WIDTH, D_CONV), CONV_WIDTH ** -0.5),
        "conv_b": n(ks[9], (DEPTH, D_CONV), 0.02),
        "ln_g": 1.0 + n(ks[10], (DEPTH, D_CONV), 0.02),
        "ln_b": n(ks[11], (DEPTH, D_CONV), 0.02),
        "short_w": n(ks[12], (DEPTH, SHORT_WIDTH, D_SHORT), SHORT_WIDTH ** -0.5),
        "w_out": n(ks[13], (DEPTH, d, d), d ** -0.5),
        "g_norm2": 1.0 + n(ks[14], (DEPTH, d), 0.02),
        "w_query": n(ks[15], (DEPTH, d, PEER_HEADS * D_QUERY), d ** -0.5),
        "sub_keys": n(ks[16], (DEPTH, PEER_HEADS, 2, N_KEYS, D_HALF), D_HALF ** -0.5),
        "expert_u": n(ks[17], (DEPTH, N_EXPERTS, d), d ** -0.5),
        "expert_v": n(ks[18], (DEPTH, N_EXPERTS, d), PEER_HEADS ** -0.5),
        "g_final": 1.0 + n(ks[19], (d,), 0.02),
    }


def reference(x_prompt, x_sample, c_prompt, c_sample, w_ada, b_ada, g_norm1, w_in, conv_w, conv_b,
              ln_g, ln_b, short_w, w_out, g_norm2, w_query, sub_keys, expert_u, expert_v, g_final):
    y_prompt = trunk(x_prompt, c_prompt, w_ada, b_ada, g_norm1, w_in, conv_w, conv_b, ln_g, ln_b,
                     short_w, w_out, g_norm2, w_query, sub_keys, expert_u, expert_v, g_final)
    y_sample = trunk(x_sample, c_sample, w_ada, b_ada, g_norm1, w_in, conv_w, conv_b, ln_g, ln_b,
                     short_w, w_out, g_norm2, w_query, sub_keys, expert_u, expert_v, g_final)
    return (y_prompt, y_sample)
```

```python
import functools
import math

import jax
import jax.numpy as jnp
from jax import lax
from jax.experimental import pallas as pl
from jax.experimental.pallas import tpu as pltpu

F32 = jnp.float32
BF16 = jnp.bfloat16

RMS_EPS = 1e-6
LN_EPS = 1e-5
TOPK = 16
PEER_HEADS = 8
N_KEYS = 128
CONV_HALO = 16
LANES = 128
VMEM_LIMIT = 56 * 1024 * 1024

_NT = (((1,), (1,)), ((), ()))
_TN = (((0,), (0,)), ((), ()))


def _rmsnorm(x, g):
    ms = jnp.mean(x * x, axis=-1, keepdims=True)
    return x * lax.rsqrt(ms + RMS_EPS) * g


def _params(sem):
    return pltpu.CompilerParams(dimension_semantics=sem, vmem_limit_bytes=VMEM_LIMIT)


def _const_spec(shape):
    nd = len(shape)
    return pl.BlockSpec(shape, lambda *_: (0,) * nd, pipeline_mode=pl.Buffered(1))


def _adaln_kernel(c_ref, w_ref, b_ref, o_ref):
    c = c_ref[...]
    o_ref[...] = jnp.dot(c * jax.nn.sigmoid(c), w_ref[...], preferred_element_type=F32) + b_ref[...]


def _adaln(c, w, b):
    r, d = c.shape
    n = w.shape[1]
    tn = next(c for c in (1024, 512, 256, LANES) if n % c == 0)
    return pl.pallas_call(
        _adaln_kernel,
        grid=(n // tn,),
        in_specs=[pl.BlockSpec((r, d), lambda j: (0, 0)),
                  pl.BlockSpec((d, tn), lambda j: (0, j)),
                  pl.BlockSpec((1, tn), lambda j: (0, j))],
        out_specs=pl.BlockSpec((r, tn), lambda j: (0, j)),
        out_shape=jax.ShapeDtypeStruct((r, n), F32),
        compiler_params=_params(("arbitrary",)),
        name="adaln",
    )(c, w, b)


def _inproj_kernel(x_ref, g_ref, sc_ref, sh_ref, w_ref, z_ref, h_ref):
    @pl.when(pl.program_id(1) == 0)
    def _():
        h = _rmsnorm(x_ref[...], g_ref[...]) * (1.0 + sc_ref[0]) + sh_ref[0]
        h_ref[...] = h.astype(BF16)

    z_ref[...] = jnp.dot(h_ref[...], w_ref[...], preferred_element_type=F32)


def _inproj(x2, g, sc, sh, w, seq):
    t, d = x2.shape
    n = w.shape[1]
    tm = min(512, seq)
    tn = n // 5 if (n // 5) % LANES == 0 else n
    bpb = seq // tm
    return pl.pallas_call(
        _inproj_kernel,
        grid=(t // tm, n // tn),
        in_specs=[pl.BlockSpec((tm, d), lambda i, j: (i, 0)),
                  pl.BlockSpec((1, d), lambda i, j: (0, 0)),
                  pl.BlockSpec((1, 1, d), lambda i, j: (i // bpb, 0, 0)),
                  pl.BlockSpec((1, 1, d), lambda i, j: (i // bpb, 0, 0)),
                  pl.BlockSpec((d, tn), lambda i, j: (0, j))],
        out_specs=pl.BlockSpec((tm, tn), lambda i, j: (i, j)),
        out_shape=jax.ShapeDtypeStruct((t, n), F32),
        scratch_shapes=[pltpu.VMEM((tm, d), BF16)],
        compiler_params=_params(("parallel", "arbitrary")),
        name="in_proj",
    )(x2, g, sc, sh, w)


def _mixer_kernel(z_ref, zp_ref, zn_ref, x_ref, gt_ref, cw_ref, cb_ref, lg_ref, lb_ref, sw_ref,
                  wo_ref, o_ref, ubuf, vbuf, cbuf, lhs, *, dc, dsh, conv_width, short_width):
    s = pl.program_id(1)
    ns = pl.num_programs(1)
    sb = x_ref.shape[1]
    halo = CONV_HALO
    rows = 64 if sb % 64 == 0 else sb

    def glu(zz):
        return zz[:, :dc] * jax.nn.sigmoid(zz[:, dc:2 * dc])

    def cx(zz):
        return zz[:, 2 * dc + dsh:2 * dc + 2 * dsh] * zz[:, 2 * dc + 2 * dsh:]

    zp = jnp.where(s > 0, zp_ref[0], 0.0)
    zn = jnp.where(s < ns - 1, zn_ref[0], 0.0)
    ubuf[0:halo, :] = glu(zp)
    ubuf[halo + sb:, :] = glu(zn)
    vbuf[0:halo, :] = cx(zp)
    vbuf[halo + sb:, :] = cx(zn)
    for r0 in range(0, sb, rows):
        zz = z_ref[0, r0:r0 + rows, :]
        ubuf[halo + r0:halo + r0 + rows, :] = glu(zz)
        vbuf[halo + r0:halo + r0 + rows, :] = cx(zz)

    cpad = (conv_width - 1) // 2

    def conv_body(ct, carry):
        c0 = pl.multiple_of(ct * LANES, LANES)
        cs = pl.ds(c0, LANES)
        for r0 in range(0, sb, rows):
            acc = jnp.broadcast_to(cb_ref[:, cs], (rows, LANES))
            for k in range(conv_width):
                start = halo + r0 - cpad + k
                acc = acc + cw_ref[k:k + 1, cs] * ubuf[start:start + rows, cs]
            cbuf[r0:r0 + rows, cs] = acc
        return carry

    lax.fori_loop(0, dc // LANES, conv_body, 0)

    spad = (short_width - 1) // 2
    for r0 in range(0, sb, rows):
        y = cbuf[r0:r0 + rows, :]
        mu = jnp.mean(y, axis=-1, keepdims=True)
        yc = y - mu
        var = jnp.mean(yc * yc, axis=-1, keepdims=True)
        yn = yc * lax.rsqrt(var + LN_EPS) * lg_ref[...] + lb_ref[...]
        lhs[r0:r0 + rows, 0:dc] = (yn * jax.nn.sigmoid(yn)).astype(BF16)
        sacc = jnp.zeros((rows, dsh), F32)
        for k in range(short_width):
            start = halo + r0 - spad + k
            sacc = sacc + sw_ref[k:k + 1, :] * vbuf[start:start + rows, :]
        bg = z_ref[0, r0:r0 + rows, 2 * dc:2 * dc + dsh]
        lhs[r0:r0 + rows, dc:dc + dsh] = (bg * sacc).astype(BF16)

    mix = jnp.dot(lhs[...], wo_ref[...], preferred_element_type=F32)
    o_ref[0] = x_ref[0] + gt_ref[0] * mix


def _mixer(z3, x3, gt, conv_w, conv_b, ln_g, ln_b, short_w, w_out):
    b, seq, d = x3.shape
    d_in = z3.shape[-1]
    dc = conv_w.shape[1]
    dsh = short_w.shape[1]
    sb = min(256, seq)
    halo = CONV_HALO
    hpb = sb // halo
    n_halo = seq // halo
    kern = functools.partial(_mixer_kernel, dc=dc, dsh=dsh, conv_width=conv_w.shape[0],
                             short_width=short_w.shape[0])
    return pl.pallas_call(
        kern,
        grid=(b, seq // sb),
        in_specs=[pl.BlockSpec((1, sb, d_in), lambda i, j: (i, j, 0)),
                  pl.BlockSpec((1, halo, d_in), lambda i, j: (i, jnp.maximum(j * hpb - 1, 0), 0)),
                  pl.BlockSpec((1, halo, d_in),
                               lambda i, j: (i, jnp.minimum((j + 1) * hpb, n_halo - 1), 0)),
                  pl.BlockSpec((1, sb, d), lambda i, j: (i, j, 0)),
                  pl.BlockSpec((1, 1, d), lambda i, j: (i, 0, 0)),
                  _const_spec(conv_w.shape), _const_spec(conv_b.shape), _const_spec(ln_g.shape),
                  _const_spec(ln_b.shape), _const_spec(short_w.shape), _const_spec(w_out.shape)],
        out_specs=pl.BlockSpec((1, sb, d), lambda i, j: (i, j, 0)),
        out_shape=jax.ShapeDtypeStruct((b, seq, d), F32),
        scratch_shapes=[pltpu.VMEM((sb + 2 * halo, dc), F32),
                        pltpu.VMEM((sb + 2 * halo, dsh), F32),
                        pltpu.VMEM((sb, dc), F32),
                        pltpu.VMEM((sb, dc + dsh), BF16)],
        compiler_params=_params(("parallel", "arbitrary")),
        name="mixer",
    )(z3, z3, z3, x3, gt, conv_w, conv_b, ln_g, ln_b, short_w, w_out)


def _staircase(k):
    return [(a, b) for a in range(k) for b in range(k) if (a + 1) * (b + 1) <= k]


def _peer_kernel(x_ref, g2_ref, sc_ref, sh_ref, gt_ref, gf_ref, wq_ref, keys_ref, u_ref, v_ref,
                 o_ref, hb_ref, s_ref, vals_ref, stat_ref, th_ref, w_ref, g1_ref, act_ref,
                 coef_ref, *, heads, n_keys, topk):
    j = pl.program_id(1)
    nj = pl.num_programs(1)
    tb = x_ref.shape[0]
    eb = u_ref.shape[0]
    dh = keys_ref.shape[2]
    n_lt = tb // LANES
    ni = eb // n_keys
    kk = topk + 1

    @pl.when(j == 0)
    def _prologue():
        h = _rmsnorm(x_ref[...], g2_ref[...]) * (1.0 + sc_ref[0]) + sh_ref[0]
        hb_ref[...] = h.astype(BF16)
        for hd in range(heads):
            q = jnp.dot(hb_ref[...], wq_ref[:, 2 * hd * dh:2 * (hd + 1) * dh],
                        preferred_element_type=F32).astype(BF16)
            for p in range(2):
                s_ref[2 * hd + p] = lax.dot_general(keys_ref[2 * hd + p], q[:, p * dh:(p + 1) * dh],
                                                    _NT, preferred_element_type=F32)

        def top_body(lt, carry):
            ls = pl.ds(pl.multiple_of(lt * LANES, LANES), LANES)
            for hp in range(2 * heads):
                cur = s_ref[hp, :, ls]
                for a in range(kk):
                    m = jnp.max(cur, axis=0, keepdims=True)
                    vals_ref[hp % 2, a, hp // 2:hp // 2 + 1, ls] = m
                    if a + 1 < kk:
                        cur = jnp.where(cur == m, -jnp.inf, cur)
            return carry

        lax.fori_loop(0, n_lt, top_body, 0)

        pairs = _staircase(kk)

        def cand_body(lt, carry):
            ls = pl.ds(pl.multiple_of(lt * LANES, LANES), LANES)
            v0 = [vals_ref[0, a, :, ls] for a in range(kk)]
            v1 = [vals_ref[1, a, :, ls] for a in range(kk)]
            cands = [v0[a] + v1[b] for a, b in pairs]
            best = []
            for r in range(kk):
                m = functools.reduce(jnp.maximum, cands)
                best.append(m)
                if r + 1 < kk:
                    cands = [jnp.where(c == m, -jnp.inf, c) for c in cands]
            z = functools.reduce(lambda p, c: p + c, [jnp.exp(c - best[0]) for c in best[:topk]])
            stat_ref[0, :, ls] = 0.5 * (best[topk - 1] + best[topk])
            stat_ref[1, :, ls] = 1.0 / z
            return carry

        lax.fori_loop(0, n_lt, cand_body, 0)

        for hd in range(heads):
            s0 = s_ref[2 * hd]
            s1 = s_ref[2 * hd + 1]
            th = stat_ref[0, hd:hd + 1, :] - s0
            w = jnp.exp(s0 - vals_ref[0, 0, hd:hd + 1, :]) * stat_ref[1, hd:hd + 1, :]
            for g in range(n_keys // ni):
                th_ref[hd, g, 0:ni, :] = th[g * ni:(g + 1) * ni, :]
                w_ref[hd, g, 0:ni, :] = w[g * ni:(g + 1) * ni, :]
            g1_ref[hd] = jnp.exp(s1 - vals_ref[1, 0, hd:hd + 1, :])

    act_ref[...] = lax.dot_general(u_ref[...], hb_ref[...], _NT, preferred_element_type=F32)

    def gate_body(lt, carry):
        ls = pl.ds(pl.multiple_of(lt * LANES, LANES), LANES)
        thr = [th_ref[hd, j, :, ls] for hd in range(heads)]
        wr = [w_ref[hd, j, :, ls] for hd in range(heads)]
        for ii in range(ni):
            gate = jnp.zeros((n_keys, LANES), F32)
            for hd in range(heads):
                sel = s_ref[2 * hd + 1, :, ls] >= thr[hd][ii:ii + 1, :]
                gate = gate + jnp.where(sel, g1_ref[hd, :, ls] * wr[hd][ii:ii + 1, :], 0.0)
            a = act_ref[ii * n_keys:(ii + 1) * n_keys, ls]
            gelu = 0.5 * a * (1.0 + lax.erf(a * (1.0 / math.sqrt(2.0))))
            coef_ref[ii * n_keys:(ii + 1) * n_keys, ls] = (gelu * gate).astype(BF16)
        return carry

    lax.fori_loop(0, n_lt, gate_body, 0)

    contrib = lax.dot_general(coef_ref[...], v_ref[...], _TN, preferred_element_type=F32)

    @pl.when(j == 0)
    def _():
        o_ref[...] = contrib

    @pl.when(j > 0)
    def _():
        o_ref[...] += contrib

    @pl.when(j == nj - 1)
    def _():
        y = x_ref[...] + gt_ref[0] * o_ref[...]
        o_ref[...] = _rmsnorm(y, gf_ref[...])


def _peer(x2, g2, sc, sh, gt, gf, wq, keys, eu, ev, seq):
    t, d = x2.shape
    n_exp = eu.shape[0]
    n_hp, n_keys, dh = keys.shape
    heads = n_hp // 2
    tb = min(512, seq)
    eb = 512
    bpb = seq // tb
    kern = functools.partial(_peer_kernel, heads=heads, n_keys=n_keys, topk=TOPK)
    mod_spec = pl.BlockSpec((1, 1, d), lambda i, j: (i // bpb, 0, 0))
    return pl.pallas_call(
        kern,
        grid=(t // tb, n_exp // eb),
        in_specs=[pl.BlockSpec((tb, d), lambda i, j: (i, 0), pipeline_mode=pl.Buffered(1)),
                  _const_spec(g2.shape), mod_spec, mod_spec, mod_spec, _const_spec(gf.shape),
                  _const_spec(wq.shape), _const_spec(keys.shape),
                  pl.BlockSpec((eb, d), lambda i, j: (j, 0)),
                  pl.BlockSpec((eb, d), lambda i, j: (j, 0))],
        out_specs=pl.BlockSpec((tb, d), lambda i, j: (i, 0)),
        out_shape=jax.ShapeDtypeStruct((t, d), F32),
        scratch_shapes=[pltpu.VMEM((tb, d), BF16),
                        pltpu.VMEM((n_hp, n_keys, tb), F32),
                        pltpu.VMEM((2, TOPK + 1, heads, tb), F32),
                        pltpu.VMEM((2, heads, tb), F32),
                        pltpu.VMEM((heads, n_exp // eb, 8, tb), F32),
                        pltpu.VMEM((heads, n_exp // eb, 8, tb), F32),
                        pltpu.VMEM((heads, n_keys, tb), F32),
                        pltpu.VMEM((eb, tb), F32),
                        pltpu.VMEM((eb, tb), BF16)],
        compiler_params=_params(("parallel", "arbitrary")),
        name="peer",
    )(x2, g2, sc, sh, gt, gf, wq, keys, eu, ev)


def _trunk(x, mod, g_norm1, w_in, conv_w, conv_b, ln_g, ln_b, short_w, w_out, g_norm2, w_query,
           keys, eu, ev, g_final):
    b, seq, d = x.shape
    sh1, sc1, gt1, sh2, sc2, gt2 = [m[:, None, :] for m in jnp.split(mod, 6, axis=-1)]
    x2 = x.reshape(b * seq, d)
    z = _inproj(x2, g_norm1, sc1, sh1, w_in, seq)
    x1 = _mixer(z.reshape(b, seq, -1), x, gt1, conv_w, conv_b, ln_g, ln_b, short_w, w_out)
    y = _peer(x1.reshape(b * seq, d), g_norm2, sc2, sh2, gt2, g_final, w_query, keys, eu, ev, seq)
    return y.reshape(b, seq, d)


def kernel(x_prompt, x_sample, c_prompt, c_sample, w_ada, b_ada, g_norm1, w_in, conv_w, conv_b,
           ln_g, ln_b, short_w, w_out, g_norm2, w_query, sub_keys, expert_u, expert_v, g_final):
    assert w_ada.shape[0] == 1, "one layer"
    bp, bs = c_prompt.shape[0], c_sample.shape[0]
    rows = -(-(bp + bs) // 8) * 8
    c_all = jnp.concatenate([c_prompt, c_sample, jnp.zeros((rows - bp - bs, c_prompt.shape[1]), F32)])
    mod = _adaln(c_all, w_ada[0], b_ada)
    n_hp = sub_keys.shape[1] * sub_keys.shape[2]
    weights = (g_norm1, w_in[0].astype(BF16), conv_w[0], conv_b, ln_g, ln_b, short_w[0],
               w_out[0].astype(BF16), g_norm2, w_query[0].astype(BF16),
               sub_keys[0].reshape(n_hp, sub_keys.shape[3], sub_keys.shape[4]).astype(BF16),
               expert_u[0].astype(BF16), expert_v[0].astype(BF16), g_final[None, :])
    y_prompt = _trunk(x_prompt, mod[:bp], *weights)
    y_sample = _trunk(x_sample, mod[bp:bp + bs], *weights)
    return (y_prompt, y_sample)
```

```python
import functools
import math

import jax
import jax.numpy as jnp
from jax import lax
from jax.experimental import pallas as pl
from jax.experimental.pallas import tpu as pltpu

F32 = jnp.float32
BF16 = jnp.bfloat16

RMS_EPS = 1e-6
LN_EPS = 1e-5
TOPK = 16
PEER_HEADS = 8
N_KEYS = 128
CONV_HALO = 16
LANES = 128
VMEM_LIMIT = 56 * 1024 * 1024

_NT = (((1,), (1,)), ((), ()))
_TN = (((0,), (0,)), ((), ()))


def _rmsnorm(x, g):
    ms = jnp.mean(x * x, axis=-1, keepdims=True)
    return x * lax.rsqrt(ms + RMS_EPS) * g


def _params(sem, flags=None):
    return pltpu.CompilerParams(dimension_semantics=sem, vmem_limit_bytes=VMEM_LIMIT, flags=flags)


def _const_spec(shape):
    nd = len(shape)
    return pl.BlockSpec(shape, lambda *_: (0,) * nd, pipeline_mode=pl.Buffered(1))


def _adaln_kernel(c_ref, w_ref, b_ref, o_ref):
    c = c_ref[...]
    o_ref[...] = jnp.dot(c * jax.nn.sigmoid(c), w_ref[...], preferred_element_type=F32) + b_ref[...]


def _adaln(c, w, b):
    r, d = c.shape
    n = w.shape[1]
    tn = next(c for c in (1024, 512, 256, LANES) if n % c == 0)
    return pl.pallas_call(
        _adaln_kernel,
        grid=(n // tn,),
        in_specs=[pl.BlockSpec((r, d), lambda j: (0, 0)),
                  pl.BlockSpec((d, tn), lambda j: (0, j)),
                  pl.BlockSpec((1, tn), lambda j: (0, j))],
        out_specs=pl.BlockSpec((r, tn), lambda j: (0, j)),
        out_shape=jax.ShapeDtypeStruct((r, n), F32),
        compiler_params=_params(("arbitrary",)),
        name="adaln",
    )(c, w, b)


def _inproj_kernel(x_ref, g_ref, sc_ref, sh_ref, w_ref, z_ref, h_ref):
    @pl.when(pl.program_id(1) == 0)
    def _():
        h = _rmsnorm(x_ref[...], g_ref[...]) * (1.0 + sc_ref[0]) + sh_ref[0]
        h_ref[...] = h.astype(BF16)

    z_ref[...] = jnp.dot(h_ref[...], w_ref[...], preferred_element_type=F32)


def _inproj(x2, g, sc, sh, w, seq):
    t, d = x2.shape
    n = w.shape[1]
    tm = min(1024, seq)
    tn = n // 5 if (n // 5) % LANES == 0 else n
    bpb = seq // tm
    return pl.pallas_call(
        _inproj_kernel,
        grid=(t // tm, n // tn),
        in_specs=[pl.BlockSpec((tm, d), lambda i, j: (i, 0)),
                  pl.BlockSpec((1, d), lambda i, j: (0, 0)),
                  pl.BlockSpec((1, 1, d), lambda i, j: (i // bpb, 0, 0)),
                  pl.BlockSpec((1, 1, d), lambda i, j: (i // bpb, 0, 0)),
                  pl.BlockSpec((d, tn), lambda i, j: (0, j))],
        out_specs=pl.BlockSpec((tm, tn), lambda i, j: (i, j)),
        out_shape=jax.ShapeDtypeStruct((t, n), F32),
        scratch_shapes=[pltpu.VMEM((tm, d), BF16)],
        compiler_params=_params(("parallel", "arbitrary")),
        name="in_proj",
    )(x2, g, sc, sh, w)


def _mixer_kernel(z_ref, zp_ref, zn_ref, x_ref, gt_ref, cw_ref, cb_ref, lg_ref, lb_ref, sw_ref,
                  wo_ref, o_ref, ubuf, vbuf, cbuf, lhs, *, dc, dsh, conv_width, short_width):
    s = pl.program_id(1)
    ns = pl.num_programs(1)
    sb = x_ref.shape[1]
    halo = CONV_HALO
    rows = 64 if sb % 64 == 0 else sb

    def glu(zz):
        return zz[:, :dc] * jax.nn.sigmoid(zz[:, dc:2 * dc])

    def cx(zz):
        return zz[:, 2 * dc + dsh:2 * dc + 2 * dsh] * zz[:, 2 * dc + 2 * dsh:]

    zp = jnp.where(s > 0, zp_ref[0], 0.0)
    zn = jnp.where(s < ns - 1, zn_ref[0], 0.0)
    ubuf[0:halo, :] = glu(zp)
    ubuf[halo + sb:, :] = glu(zn)
    vbuf[0:halo, :] = cx(zp)
    vbuf[halo + sb:, :] = cx(zn)
    for r0 in range(0, sb, rows):
        zz = z_ref[0, r0:r0 + rows, :]
        ubuf[halo + r0:halo + r0 + rows, :] = glu(zz)
        vbuf[halo + r0:halo + r0 + rows, :] = cx(zz)

    cpad = (conv_width - 1) // 2

    base = halo - cpad
    assert 0 <= base and base + conv_width - 1 <= 2 * halo and rows % 8 == 0

    def conv_body(ct, carry):
        c0 = pl.multiple_of(ct * LANES, LANES)
        cs = pl.ds(c0, LANES)
        for r0 in range(0, sb, rows):
            acc = jnp.broadcast_to(cb_ref[:, cs], (rows, LANES))
            for s in range(8):
                part = None
                for k in range(conv_width):
                    if (base + k) % 8 != s:
                        continue
                    a8 = base + k - s
                    term = cw_ref[k:k + 1, cs] * ubuf[r0 + a8:r0 + a8 + rows + 8, cs]
                    part = term if part is None else part + term
                if part is not None:
                    acc = acc + part[s:s + rows, :]
            cbuf[r0:r0 + rows, cs] = acc
        return carry

    lax.fori_loop(0, dc // LANES, conv_body, 0)

    spad = (short_width - 1) // 2
    for r0 in range(0, sb, rows):
        y = cbuf[r0:r0 + rows, :]
        mu = jnp.mean(y, axis=-1, keepdims=True)
        yc = y - mu
        var = jnp.mean(yc * yc, axis=-1, keepdims=True)
        yn = yc * lax.rsqrt(var + LN_EPS) * lg_ref[...] + lb_ref[...]
        lhs[r0:r0 + rows, 0:dc] = (yn * jax.nn.sigmoid(yn)).astype(BF16)
        sacc = jnp.zeros((rows, dsh), F32)
        for k in range(short_width):
            start = halo + r0 - spad + k
            sacc = sacc + sw_ref[k:k + 1, :] * vbuf[start:start + rows, :]
        bg = z_ref[0, r0:r0 + rows, 2 * dc:2 * dc + dsh]
        lhs[r0:r0 + rows, dc:dc + dsh] = (bg * sacc).astype(BF16)

    mix = jnp.dot(lhs[...], wo_ref[...], preferred_element_type=F32)
    o_ref[0] = x_ref[0] + gt_ref[0] * mix


def _mixer(z3, x3, gt, conv_w, conv_b, ln_g, ln_b, short_w, w_out):
    b, seq, d = x3.shape
    d_in = z3.shape[-1]
    dc = conv_w.shape[1]
    dsh = short_w.shape[1]
    sb = min(256, seq)
    halo = CONV_HALO
    hpb = sb // halo
    n_halo = seq // halo
    kern = functools.partial(_mixer_kernel, dc=dc, dsh=dsh, conv_width=conv_w.shape[0],
                             short_width=short_w.shape[0])
    return pl.pallas_call(
        kern,
        grid=(b, seq // sb),
        in_specs=[pl.BlockSpec((1, sb, d_in), lambda i, j: (i, j, 0)),
                  pl.BlockSpec((1, halo, d_in), lambda i, j: (i, jnp.maximum(j * hpb - 1, 0), 0)),
                  pl.BlockSpec((1, halo, d_in),
                               lambda i, j: (i, jnp.minimum((j + 1) * hpb, n_halo - 1), 0)),
                  pl.BlockSpec((1, sb, d), lambda i, j: (i, j, 0)),
                  pl.BlockSpec((1, 1, d), lambda i, j: (i, 0, 0)),
                  _const_spec(conv_w.shape), _const_spec(conv_b.shape), _const_spec(ln_g.shape),
                  _const_spec(ln_b.shape), _const_spec(short_w.shape), _const_spec(w_out.shape)],
        out_specs=pl.BlockSpec((1, sb, d), lambda i, j: (i, j, 0)),
        out_shape=jax.ShapeDtypeStruct((b, seq, d), F32),
        scratch_shapes=[pltpu.VMEM((sb + 2 * halo, dc), F32),
                        pltpu.VMEM((sb + 2 * halo, dsh), F32),
                        pltpu.VMEM((sb, dc), F32),
                        pltpu.VMEM((sb, dc + dsh), BF16)],
        compiler_params=_params(("parallel", "arbitrary")),
        name="mixer",
    )(z3, z3, z3, x3, gt, conv_w, conv_b, ln_g, ln_b, short_w, w_out)


def _staircase(k):
    return [(a, b) for a in range(k) for b in range(k) if (a + 1) * (b + 1) <= k]


def _sorting_network(n):
    pairs = []
    p = 1
    while p < n:
        k = p
        while k >= 1:
            for j in range(k % p, n - k, 2 * k):
                for i in range(min(k, n - j - k)):
                    if (i + j) // (2 * p) == (i + j + k) // (2 * p):
                        pairs.append((i + j, i + j + k))
            k //= 2
        p *= 2
    return pairs


def _peer_kernel(x_ref, g2_ref, sc_ref, sh_ref, gt_ref, gf_ref, wq_ref, keys_ref, eu_hbm, ev_hbm,
                 o_ref, hb_ref, s_ref, vals_ref, stat_ref, th_ref, w_ref, g1_ref, act0_ref,
                 act1_ref, coef0_ref, coef1_ref, ubuf, vbuf, usem, vsem, *, heads, n_keys, topk):
    tb = x_ref.shape[0]
    eb = ubuf.shape[1]
    nb = eu_hbm.shape[0] // eb
    dh = keys_ref.shape[2]
    n_lt = tb // LANES
    ni = eb // n_keys
    kk = topk + 1
    assert nb % 2 == 0 and 2 * ni == 8, "two expert blocks share one 8-row tile of per-key rows"

    def u_copy(blk, slot):
        return pltpu.make_async_copy(eu_hbm.at[pl.ds(blk * eb, eb), :], ubuf.at[slot], usem.at[slot])

    def v_copy(blk, slot):
        return pltpu.make_async_copy(ev_hbm.at[pl.ds(blk * eb, eb), :], vbuf.at[slot], vsem.at[slot])

    u_copy(0, 0).start()

    def _prologue():
        h = _rmsnorm(x_ref[...], g2_ref[...]) * (1.0 + sc_ref[0]) + sh_ref[0]
        hb_ref[...] = h.astype(BF16)
        qw = 4 * dh
        for c in range(2 * heads * dh // qw):
            q = jnp.dot(hb_ref[...], wq_ref[:, c * qw:(c + 1) * qw],
                        preferred_element_type=F32).astype(BF16)
            for r in range(qw // dh):
                hp = c * (qw // dh) + r
                s_ref[hp] = lax.dot_general(keys_ref[hp], q[:, r * dh:(r + 1) * dh], _NT,
                                            preferred_element_type=F32)

        n_tiles = n_keys // 8
        network = _sorting_network(n_tiles)

        def top_body(lt, carry):
            ls = pl.ds(pl.multiple_of(lt * LANES, LANES), LANES)
            for hp in range(2 * heads):
                col = [s_ref[hp, 8 * k:8 * (k + 1), ls] for k in range(n_tiles)]
                for a, b in network:
                    col[a], col[b] = jnp.maximum(col[a], col[b]), jnp.minimum(col[a], col[b])
                for r in range(kk):
                    m = jnp.max(col[0], axis=0, keepdims=True)
                    vals_ref[hp % 2, r, hp // 2:hp // 2 + 1, ls] = m
                    pop = col[0] == m
                    for k in range(min(kk - 1 - r, n_tiles)):
                        below = col[k + 1] if k + 1 < n_tiles else -jnp.inf
                        col[k] = jnp.where(pop, below, col[k])
            return carry

        lax.fori_loop(0, n_lt, top_body, 0)

        pairs = _staircase(kk)

        def cand_body(lt, carry):
            ls = pl.ds(pl.multiple_of(lt * LANES, LANES), LANES)
            v0 = [vals_ref[0, a, :, ls] for a in range(kk)]
            v1 = [vals_ref[1, a, :, ls] for a in range(kk)]
            cands = [v0[a] + v1[b] for a, b in pairs]
            best = []
            for r in range(kk):
                m = functools.reduce(jnp.maximum, cands)
                best.append(m)
                if r + 1 < kk:
                    cands = [jnp.where(c == m, -jnp.inf, c) for c in cands]
            z = functools.reduce(lambda p, c: p + c, [jnp.exp(c - best[0]) for c in best[:topk]])
            stat_ref[0, :, ls] = 0.5 * (best[topk - 1] + best[topk])
            stat_ref[1, :, ls] = 1.0 / z
            return carry

        lax.fori_loop(0, n_lt, cand_body, 0, unroll=2)

        for hd in range(heads):
            s0 = s_ref[2 * hd]
            s1 = s_ref[2 * hd + 1]
            m1 = vals_ref[1, 0, hd:hd + 1, :]
            th_ref[hd] = jnp.exp(stat_ref[0, hd:hd + 1, :] - s0 - m1)
            w_ref[hd] = jnp.exp(s0 - vals_ref[0, 0, hd:hd + 1, :]) * stat_ref[1, hd:hd + 1, :]
            g1 = jnp.exp(s1 - m1)
            for lt in range(n_lt):
                g1_ref[hd, lt] = g1[:, lt * LANES:(lt + 1) * LANES]

        o_ref[...] = jnp.zeros(o_ref.shape, F32)

    _prologue()

    acts = (act0_ref, act1_ref)
    coefs = (coef0_ref, coef1_ref)

    def scores(par):
        acts[par][...] = lax.dot_general(ubuf[par], hb_ref[...], _NT, preferred_element_type=F32)

    def gates(par, blk, lane_tiles):
        row0 = (blk // 2) * 8 if isinstance(blk, int) else pl.multiple_of(lax.div(blk, 2) * 8, 8)
        off = par * ni
        for lt in lane_tiles:
            ls = slice(lt * LANES, (lt + 1) * LANES)
            thr = [th_ref[hd, pl.ds(row0, 8), ls] for hd in range(heads)]
            wr = [w_ref[hd, pl.ds(row0, 8), ls] for hd in range(heads)]
            for ii in range(ni):
                gate = jnp.zeros((n_keys, LANES), F32)
                for hd in range(heads):
                    g1 = g1_ref[hd, lt]
                    sel = g1 >= thr[hd][off + ii:off + ii + 1, :]
                    gate = gate + jnp.where(sel, g1 * wr[hd][off + ii:off + ii + 1, :], 0.0)
                a = acts[par][ii * n_keys:(ii + 1) * n_keys, ls]
                gelu = 0.5 * a * (1.0 + lax.erf(a * (1.0 / math.sqrt(2.0))))
                coefs[par][ii * n_keys:(ii + 1) * n_keys, ls] = (gelu * gate).astype(BF16)

    def outputs(par):
        o_ref[...] += lax.dot_general(coefs[par][...], vbuf[par], _TN, preferred_element_type=F32)

    def slot(k, par):
        static = isinstance(k, int)
        do_scores = (k < nb) if static else True
        do_gates = (1 <= k <= nb) if static else True
        do_outputs = (k >= 2) if static else True
        if do_scores:
            u_copy(k, par).wait()
            if static:
                if k + 1 < nb:
                    u_copy(k + 1, 1 - par).start()
            else:
                @pl.when(k + 1 < nb)
                def _():
                    u_copy(k + 1, 1 - par).start()
        if do_outputs:
            v_copy(k - 2, par).wait()
        if (not static) or (1 <= k <= nb):
            v_copy(k - 1, 1 - par).start()
        if do_scores:
            scores(par)
        if do_gates:
            gates(1 - par, k - 1, range(0, n_lt // 2))
        if do_outputs:
            outputs(par)
        if do_gates:
            gates(1 - par, k - 1, range(n_lt // 2, n_lt))

    slot(0, 0)
    slot(1, 1)

    def steady(m, carry):
        slot(2 * m, 0)
        slot(2 * m + 1, 1)
        return carry

    lax.fori_loop(1, nb // 2, steady, 0)
    slot(nb, 0)
    slot(nb + 1, 1)

    y = x_ref[...] + gt_ref[0] * o_ref[...]
    o_ref[...] = _rmsnorm(y, gf_ref[...])


def _peer(x2, g2, sc, sh, gt, gf, wq, keys, eu, ev, seq):
    t, d = x2.shape
    n_exp = eu.shape[0]
    n_hp, n_keys, dh = keys.shape
    heads = n_hp // 2
    tb = min(512, seq)
    eb = 512
    bpb = seq // tb
    assert n_exp % (2 * eb) == 0 and eu.shape == ev.shape == (n_exp, d)
    kern = functools.partial(_peer_kernel, heads=heads, n_keys=n_keys, topk=TOPK)
    mod_spec = pl.BlockSpec((1, 1, d), lambda i: (i // bpb, 0, 0))
    hbm_spec = pl.BlockSpec(memory_space=pl.ANY)
    return pl.pallas_call(
        kern,
        grid=(t // tb,),
        in_specs=[pl.BlockSpec((tb, d), lambda i: (i, 0), pipeline_mode=pl.Buffered(1)),
                  _const_spec(g2.shape), mod_spec, mod_spec, mod_spec, _const_spec(gf.shape),
                  _const_spec(wq.shape), _const_spec(keys.shape), hbm_spec, hbm_spec],
        out_specs=pl.BlockSpec((tb, d), lambda i: (i, 0)),
        out_shape=jax.ShapeDtypeStruct((t, d), F32),
        scratch_shapes=[pltpu.VMEM((tb, d), BF16),
                        pltpu.VMEM((n_hp, n_keys, tb), F32),
                        pltpu.VMEM((2, TOPK + 1, heads, tb), F32),
                        pltpu.VMEM((2, heads, tb), F32),
                        pltpu.VMEM((heads, n_keys, tb), F32),
                        pltpu.VMEM((heads, n_keys, tb), F32),
                        pltpu.VMEM((heads, tb // LANES, n_keys, LANES), F32),
                        pltpu.VMEM((eb, tb), F32), pltpu.VMEM((eb, tb), F32),
                        pltpu.VMEM((eb, tb), BF16), pltpu.VMEM((eb, tb), BF16),
                        pltpu.VMEM((2, eb, d), BF16), pltpu.VMEM((2, eb, d), BF16),
                        pltpu.SemaphoreType.DMA((2,)), pltpu.SemaphoreType.DMA((2,))],
        compiler_params=_params(("parallel",)),
        name="peer",
    )(x2, g2, sc, sh, gt, gf, wq, keys, eu, ev)


def _trunk(x, mod, g_norm1, w_in, conv_w, conv_b, ln_g, ln_b, short_w, w_out, g_norm2, w_query,
           keys, eu, ev, g_final):
    b, seq, d = x.shape
    sh1, sc1, gt1, sh2, sc2, gt2 = [m[:, None, :] for m in jnp.split(mod, 6, axis=-1)]
    x2 = x.reshape(b * seq, d)
    z = _inproj(x2, g_norm1, sc1, sh1, w_in, seq)
    x1 = _mixer(z.reshape(b, seq, -1), x, gt1, conv_w, conv_b, ln_g, ln_b, short_w, w_out)
    y = _peer(x1.reshape(b * seq, d), g_norm2, sc2, sh2, gt2, g_final, w_query, keys, eu, ev, seq)
    return y.reshape(b, seq, d)


def kernel(x_prompt, x_sample, c_prompt, c_sample, w_ada, b_ada, g_norm1, w_in, conv_w, conv_b,
           ln_g, ln_b, short_w, w_out, g_norm2, w_query, sub_keys, expert_u, expert_v, g_final):
    assert w_ada.shape[0] == 1, "one layer"
    bp, bs = c_prompt.shape[0], c_sample.shape[0]
    rows = -(-(bp + bs) // 8) * 8
    c_all = jnp.concatenate([c_prompt, c_sample, jnp.zeros((rows - bp - bs, c_prompt.shape[1]), F32)])
    mod = _adaln(c_all, w_ada[0], b_ada)
    n_hp = sub_keys.shape[1] * sub_keys.shape[2]
    weights = (g_norm1, w_in[0].astype(BF16), conv_w[0], conv_b, ln_g, ln_b, short_w[0],
               w_out[0].astype(BF16), g_norm2, w_query[0].astype(BF16),
               sub_keys[0].reshape(n_hp, sub_keys.shape[3], sub_keys.shape[4]).astype(BF16),
               expert_u[0].astype(BF16), expert_v[0].astype(BF16), g_final[None, :])
    y_prompt = _trunk(x_prompt, mod[:bp], *weights)
    y_sample = _trunk(x_sample, mod[bp:bp + bs], *weights)
    return (y_prompt, y_sample)
```

```python
import functools
import math

import jax
import jax.numpy as jnp
from jax import lax
from jax.experimental import pallas as pl
from jax.experimental.pallas import tpu as pltpu

F32 = jnp.float32
BF16 = jnp.bfloat16

RMS_EPS = 1e-6
LN_EPS = 1e-5
TOPK = 16
PEER_HEADS = 8
N_KEYS = 128
CONV_HALO = 16
LANES = 128
VMEM_LIMIT = 56 * 1024 * 1024
DMA_RING = 3
DMA_CHUNKS = 4

_NT = (((1,), (1,)), ((), ()))
_TN = (((0,), (0,)), ((), ()))


def _rmsnorm(x, g):
    ms = jnp.mean(x * x, axis=-1, keepdims=True)
    return x * lax.rsqrt(ms + RMS_EPS) * g


def _params(sem, flags=None):
    return pltpu.CompilerParams(dimension_semantics=sem, vmem_limit_bytes=VMEM_LIMIT, flags=flags)


def _const_spec(shape):
    nd = len(shape)
    return pl.BlockSpec(shape, lambda *_: (0,) * nd, pipeline_mode=pl.Buffered(1))


def _adaln_kernel(c_ref, w_ref, b_ref, o_ref):
    c = c_ref[...]
    o_ref[...] = jnp.dot(c * jax.nn.sigmoid(c), w_ref[...], preferred_element_type=F32) + b_ref[...]


def _adaln(c, w, b):
    r, d = c.shape
    n = w.shape[1]
    tn = next(c for c in (1024, 512, 256, LANES) if n % c == 0)
    return pl.pallas_call(
        _adaln_kernel,
        grid=(n // tn,),
        in_specs=[pl.BlockSpec((r, d), lambda j: (0, 0)),
                  pl.BlockSpec((d, tn), lambda j: (0, j)),
                  pl.BlockSpec((1, tn), lambda j: (0, j))],
        out_specs=pl.BlockSpec((r, tn), lambda j: (0, j)),
        out_shape=jax.ShapeDtypeStruct((r, n), F32),
        compiler_params=_params(("arbitrary",)),
        name="adaln",
    )(c, w, b)


def _inproj_kernel(x_ref, g_ref, sc_ref, sh_ref, w_ref, z_ref, h_ref):
    @pl.when(pl.program_id(1) == 0)
    def _():
        h = _rmsnorm(x_ref[...], g_ref[...]) * (1.0 + sc_ref[0]) + sh_ref[0]
        h_ref[...] = h.astype(BF16)

    z_ref[...] = jnp.dot(h_ref[...], w_ref[...], preferred_element_type=F32)


def _inproj(x2, g, sc, sh, w, seq):
    t, d = x2.shape
    n = w.shape[1]
    tm = min(1024, seq)
    tn = n // 5 if (n // 5) % LANES == 0 else n
    bpb = seq // tm
    return pl.pallas_call(
        _inproj_kernel,
        grid=(t // tm, n // tn),
        in_specs=[pl.BlockSpec((tm, d), lambda i, j: (i, 0)),
                  pl.BlockSpec((1, d), lambda i, j: (0, 0)),
                  pl.BlockSpec((1, 1, d), lambda i, j: (i // bpb, 0, 0)),
                  pl.BlockSpec((1, 1, d), lambda i, j: (i // bpb, 0, 0)),
                  pl.BlockSpec((d, tn), lambda i, j: (0, j))],
        out_specs=pl.BlockSpec((tm, tn), lambda i, j: (i, j)),
        out_shape=jax.ShapeDtypeStruct((t, n), F32),
        scratch_shapes=[pltpu.VMEM((tm, d), BF16)],
        compiler_params=_params(("parallel", "arbitrary")),
        name="in_proj",
    )(x2, g, sc, sh, w)


def _mixer_kernel(z_ref, zp_ref, zn_ref, x_ref, gt_ref, cw_ref, cb_ref, lg_ref, lb_ref, sw_ref,
                  wo_ref, o_ref, ubuf, vbuf, cbuf, lhs, *, dc, dsh, conv_width, short_width):
    s = pl.program_id(1)
    ns = pl.num_programs(1)
    sb = x_ref.shape[1]
    halo = CONV_HALO
    rows = 64 if sb % 64 == 0 else sb

    def glu(zz):
        return zz[:, :dc] * jax.nn.sigmoid(zz[:, dc:2 * dc])

    def cx(zz):
        return zz[:, 2 * dc + dsh:2 * dc + 2 * dsh] * zz[:, 2 * dc + 2 * dsh:]

    zp = jnp.where(s > 0, zp_ref[0], 0.0)
    zn = jnp.where(s < ns - 1, zn_ref[0], 0.0)
    ubuf[0:halo, :] = glu(zp)
    ubuf[halo + sb:, :] = glu(zn)
    vbuf[0:halo, :] = cx(zp)
    vbuf[halo + sb:, :] = cx(zn)
    for r0 in range(0, sb, rows):
        zz = z_ref[0, r0:r0 + rows, :]
        ubuf[halo + r0:halo + r0 + rows, :] = glu(zz)
        vbuf[halo + r0:halo + r0 + rows, :] = cx(zz)

    cpad = (conv_width - 1) // 2

    base = halo - cpad
    assert 0 <= base and base + conv_width - 1 <= 2 * halo and rows % 8 == 0

    def conv_body(ct, carry):
        c0 = pl.multiple_of(ct * LANES, LANES)
        cs = pl.ds(c0, LANES)
        for r0 in range(0, sb, rows):
            acc = jnp.broadcast_to(cb_ref[:, cs], (rows, LANES))
            for s in range(8):
                part = None
                for k in range(conv_width):
                    if (base + k) % 8 != s:
                        continue
                    a8 = base + k - s
                    term = cw_ref[k:k + 1, cs] * ubuf[r0 + a8:r0 + a8 + rows + 8, cs]
                    part = term if part is None else part + term
                if part is not None:
                    acc = acc + part[s:s + rows, :]
            cbuf[r0:r0 + rows, cs] = acc
        return carry

    lax.fori_loop(0, dc // LANES, conv_body, 0)

    spad = (short_width - 1) // 2
    for r0 in range(0, sb, rows):
        y = cbuf[r0:r0 + rows, :]
        mu = jnp.mean(y, axis=-1, keepdims=True)
        yc = y - mu
        var = jnp.mean(yc * yc, axis=-1, keepdims=True)
        yn = yc * lax.rsqrt(var + LN_EPS) * lg_ref[...] + lb_ref[...]
        lhs[r0:r0 + rows, 0:dc] = (yn * jax.nn.sigmoid(yn)).astype(BF16)
        sacc = jnp.zeros((rows, dsh), F32)
        for k in range(short_width):
            start = halo + r0 - spad + k
            sacc = sacc + sw_ref[k:k + 1, :] * vbuf[start:start + rows, :]
        bg = z_ref[0, r0:r0 + rows, 2 * dc:2 * dc + dsh]
        lhs[r0:r0 + rows, dc:dc + dsh] = (bg * sacc).astype(BF16)

    mix = jnp.dot(lhs[...], wo_ref[...], preferred_element_type=F32)
    o_ref[0] = x_ref[0] + gt_ref[0] * mix


def _mixer(z3, x3, gt, conv_w, conv_b, ln_g, ln_b, short_w, w_out):
    b, seq, d = x3.shape
    d_in = z3.shape[-1]
    dc = conv_w.shape[1]
    dsh = short_w.shape[1]
    sb = min(256, seq)
    halo = CONV_HALO
    hpb = sb // halo
    n_halo = seq // halo
    kern = functools.partial(_mixer_kernel, dc=dc, dsh=dsh, conv_width=conv_w.shape[0],
                             short_width=short_w.shape[0])
    return pl.pallas_call(
        kern,
        grid=(b, seq // sb),
        in_specs=[pl.BlockSpec((1, sb, d_in), lambda i, j: (i, j, 0)),
                  pl.BlockSpec((1, halo, d_in), lambda i, j: (i, jnp.maximum(j * hpb - 1, 0), 0)),
                  pl.BlockSpec((1, halo, d_in),
                               lambda i, j: (i, jnp.minimum((j + 1) * hpb, n_halo - 1), 0)),
                  pl.BlockSpec((1, sb, d), lambda i, j: (i, j, 0)),
                  pl.BlockSpec((1, 1, d), lambda i, j: (i, 0, 0)),
                  _const_spec(conv_w.shape), _const_spec(conv_b.shape), _const_spec(ln_g.shape),
                  _const_spec(ln_b.shape), _const_spec(short_w.shape), _const_spec(w_out.shape)],
        out_specs=pl.BlockSpec((1, sb, d), lambda i, j: (i, j, 0)),
        out_shape=jax.ShapeDtypeStruct((b, seq, d), F32),
        scratch_shapes=[pltpu.VMEM((sb + 2 * halo, dc), F32),
                        pltpu.VMEM((sb + 2 * halo, dsh), F32),
                        pltpu.VMEM((sb, dc), F32),
                        pltpu.VMEM((sb, dc + dsh), BF16)],
        compiler_params=_params(("parallel", "arbitrary")),
        name="mixer",
    )(z3, z3, z3, x3, gt, conv_w, conv_b, ln_g, ln_b, short_w, w_out)


def _staircase(k):
    return [(a, b) for a in range(k) for b in range(k) if (a + 1) * (b + 1) <= k]


def _sorting_network(n):
    pairs = []
    p = 1
    while p < n:
        k = p
        while k >= 1:
            for j in range(k % p, n - k, 2 * k):
                for i in range(min(k, n - j - k)):
                    if (i + j) // (2 * p) == (i + j + k) // (2 * p):
                        pairs.append((i + j, i + j + k))
            k //= 2
        p *= 2
    return pairs


def _peer_kernel(x_ref, g2_ref, sc_ref, sh_ref, gt_ref, gf_ref, wq_ref, keys_ref, eu_hbm, ev_hbm,
                 o_ref, hb_ref, s_ref, vals_ref, stat_ref, th_ref, w_ref, g1_ref, act0_ref,
                 act1_ref, coef0_ref, coef1_ref, ubuf, vbuf, usem, vsem, *, heads, n_keys, topk):
    tb = x_ref.shape[0]
    eb = ubuf.shape[1]
    nb = eu_hbm.shape[0] // eb
    dh = keys_ref.shape[2]
    n_lt = tb // LANES
    ni = eb // n_keys
    kk = topk + 1
    assert nb % 2 == 0 and 2 * ni == 8, "two expert blocks share one 8-row tile of per-key rows"

    ring = ubuf.shape[0]
    sub = eb // DMA_CHUNKS

    def block_copies(hbm, buf, sem, blk, rslot):
        return [pltpu.make_async_copy(hbm.at[pl.ds(blk * eb + c * sub, sub), :],
                                      buf.at[rslot, pl.ds(c * sub, sub), :], sem.at[rslot])
                for c in range(DMA_CHUNKS)]

    def start_block(hbm, buf, sem, blk):
        rslot = blk % ring if isinstance(blk, int) else lax.rem(blk, ring)
        for c, cp in enumerate(block_copies(hbm, buf, sem, blk, rslot)):
            cp.start(priority=c % 2)

    def wait_block(hbm, buf, sem, blk):
        rslot = blk % ring if isinstance(blk, int) else lax.rem(blk, ring)
        for cp in block_copies(hbm, buf, sem, blk, rslot):
            cp.wait()
        return rslot

    start_block(eu_hbm, ubuf, usem, 0)
    start_block(eu_hbm, ubuf, usem, 1)

    def _prologue():
        h = _rmsnorm(x_ref[...], g2_ref[...]) * (1.0 + sc_ref[0]) + sh_ref[0]
        hb_ref[...] = h.astype(BF16)
        qw = 4 * dh
        for c in range(2 * heads * dh // qw):
            q = jnp.dot(hb_ref[...], wq_ref[:, c * qw:(c + 1) * qw],
                        preferred_element_type=F32).astype(BF16)
            for r in range(qw // dh):
                hp = c * (qw // dh) + r
                s_ref[hp] = lax.dot_general(keys_ref[hp], q[:, r * dh:(r + 1) * dh], _NT,
                                            preferred_element_type=F32)

        n_tiles = n_keys // 8
        network = _sorting_network(n_tiles)

        def top_body(lt, carry):
            ls = pl.ds(pl.multiple_of(lt * LANES, LANES), LANES)
            for hp in range(2 * heads):
                col = [s_ref[hp, 8 * k:8 * (k + 1), ls] for k in range(n_tiles)]
                for a, b in network:
                    col[a], col[b] = jnp.maximum(col[a], col[b]), jnp.minimum(col[a], col[b])
                for r in range(kk):
                    m = jnp.max(col[0], axis=0, keepdims=True)
                    vals_ref[hp % 2, r, hp // 2:hp // 2 + 1, ls] = m
                    pop = col[0] == m
                    for k in range(min(kk - 1 - r, n_tiles)):
                        below = col[k + 1] if k + 1 < n_tiles else -jnp.inf
                        col[k] = jnp.where(pop, below, col[k])
            return carry

        lax.fori_loop(0, n_lt, top_body, 0)

        pairs = _staircase(kk)

        def cand_body(lt, carry):
            ls = pl.ds(pl.multiple_of(lt * LANES, LANES), LANES)
            v0 = [vals_ref[0, a, :, ls] for a in range(kk)]
            v1 = [vals_ref[1, a, :, ls] for a in range(kk)]
            cands = [v0[a] + v1[b] for a, b in pairs]
            best = []
            for r in range(kk):
                m = functools.reduce(jnp.maximum, cands)
                best.append(m)
                if r + 1 < kk:
                    cands = [jnp.where(c == m, -jnp.inf, c) for c in cands]
            z = functools.reduce(lambda p, c: p + c, [jnp.exp(c - best[0]) for c in best[:topk]])
            stat_ref[0, :, ls] = 0.5 * (best[topk - 1] + best[topk])
            stat_ref[1, :, ls] = 1.0 / z
            return carry

        lax.fori_loop(0, n_lt, cand_body, 0, unroll=2)

        for hd in range(heads):
            s0 = s_ref[2 * hd]
            s1 = s_ref[2 * hd + 1]
            m1 = vals_ref[1, 0, hd:hd + 1, :]
            th_ref[hd] = jnp.exp(stat_ref[0, hd:hd + 1, :] - s0 - m1)
            w_ref[hd] = jnp.exp(s0 - vals_ref[0, 0, hd:hd + 1, :]) * stat_ref[1, hd:hd + 1, :]
            g1 = jnp.exp(s1 - m1)
            for lt in range(n_lt):
                g1_ref[hd, lt] = g1[:, lt * LANES:(lt + 1) * LANES]

        o_ref[...] = jnp.zeros(o_ref.shape, F32)

    _prologue()

    acts = (act0_ref, act1_ref)
    coefs = (coef0_ref, coef1_ref)

    def scores(par, rslot):
        acts[par][...] = lax.dot_general(ubuf[rslot], hb_ref[...], _NT, preferred_element_type=F32)

    def gates(par, blk, lane_tiles):
        row0 = (blk // 2) * 8 if isinstance(blk, int) else pl.multiple_of(lax.div(blk, 2) * 8, 8)
        off = par * ni
        for lt in lane_tiles:
            ls = slice(lt * LANES, (lt + 1) * LANES)
            thr = [th_ref[hd, pl.ds(row0, 8), ls] for hd in range(heads)]
            wr = [w_ref[hd, pl.ds(row0, 8), ls] for hd in range(heads)]
            for ii in range(ni):
                gate = jnp.zeros((n_keys, LANES), F32)
                for hd in range(heads):
                    g1 = g1_ref[hd, lt]
                    sel = g1 >= thr[hd][off + ii:off + ii + 1, :]
                    gate = gate + jnp.where(sel, g1 * wr[hd][off + ii:off + ii + 1, :], 0.0)
                a = acts[par][ii * n_keys:(ii + 1) * n_keys, ls]
                gelu = 0.5 * a * (1.0 + lax.erf(a * (1.0 / math.sqrt(2.0))))
                coefs[par][ii * n_keys:(ii + 1) * n_keys, ls] = (gelu * gate).astype(BF16)

    def outputs(par, rslot):
        o_ref[...] += lax.dot_general(coefs[par][...], vbuf[rslot], _TN, preferred_element_type=F32)

    def slot(k, par):
        static = isinstance(k, int)
        do_scores = (k < nb) if static else True
        do_gates = (1 <= k <= nb) if static else True
        do_outputs = (k >= 2) if static else True
        if static:
            if k + 2 < nb:
                start_block(eu_hbm, ubuf, usem, k + 2)
        else:
            @pl.when(k + 2 < nb)
            def _():
                start_block(eu_hbm, ubuf, usem, k + 2)
        if do_scores:
            start_block(ev_hbm, vbuf, vsem, k)
            u_slot = wait_block(eu_hbm, ubuf, usem, k)
        if do_outputs:
            v_slot = wait_block(ev_hbm, vbuf, vsem, k - 2)
        if do_scores:
            scores(par, u_slot)
        if do_gates:
            gates(1 - par, k - 1, range(0, n_lt // 2))
        if do_outputs:
            outputs(par, v_slot)
        if do_gates:
            gates(1 - par, k - 1, range(n_lt // 2, n_lt))

    slot(0, 0)
    slot(1, 1)

    def steady(m, carry):
        slot(2 * m, 0)
        slot(2 * m + 1, 1)
        return carry

    lax.fori_loop(1, nb // 2, steady, 0)
    slot(nb, 0)
    slot(nb + 1, 1)

    y = x_ref[...] + gt_ref[0] * o_ref[...]
    o_ref[...] = _rmsnorm(y, gf_ref[...])


def _peer(x2, g2, sc, sh, gt, gf, wq, keys, eu, ev, seq):
    t, d = x2.shape
    n_exp = eu.shape[0]
    n_hp, n_keys, dh = keys.shape
    heads = n_hp // 2
    tb = min(512, seq)
    eb = 512
    bpb = seq // tb
    assert n_exp % (2 * eb) == 0 and eu.shape == ev.shape == (n_exp, d)
    kern = functools.partial(_peer_kernel, heads=heads, n_keys=n_keys, topk=TOPK)
    mod_spec = pl.BlockSpec((1, 1, d), lambda i: (i // bpb, 0, 0))
    hbm_spec = pl.BlockSpec(memory_space=pl.ANY)
    return pl.pallas_call(
        kern,
        grid=(t // tb,),
        in_specs=[pl.BlockSpec((tb, d), lambda i: (i, 0), pipeline_mode=pl.Buffered(1)),
                  _const_spec(g2.shape), mod_spec, mod_spec, mod_spec, _const_spec(gf.shape),
                  _const_spec(wq.shape), _const_spec(keys.shape), hbm_spec, hbm_spec],
        out_specs=pl.BlockSpec((tb, d), lambda i: (i, 0)),
        out_shape=jax.ShapeDtypeStruct((t, d), F32),
        scratch_shapes=[pltpu.VMEM((tb, d), BF16),
                        pltpu.VMEM((n_hp, n_keys, tb), F32),
                        pltpu.VMEM((2, TOPK + 1, heads, tb), F32),
                        pltpu.VMEM((2, heads, tb), F32),
                        pltpu.VMEM((heads, n_keys, tb), F32),
                        pltpu.VMEM((heads, n_keys, tb), F32),
                        pltpu.VMEM((heads, tb // LANES, n_keys, LANES), F32),
                        pltpu.VMEM((eb, tb), F32), pltpu.VMEM((eb, tb), F32),
                        pltpu.VMEM((eb, tb), BF16), pltpu.VMEM((eb, tb), BF16),
                        pltpu.VMEM((DMA_RING, eb, d), BF16),
                        pltpu.VMEM((DMA_RING, eb, d), BF16),
                        pltpu.SemaphoreType.DMA((DMA_RING,)), pltpu.SemaphoreType.DMA((DMA_RING,))],
        compiler_params=_params(("parallel",)),
        name="peer",
    )(x2, g2, sc, sh, gt, gf, wq, keys, eu, ev)


def _trunk(x, mod, g_norm1, w_in, conv_w, conv_b, ln_g, ln_b, short_w, w_out, g_norm2, w_query,
           keys, eu, ev, g_final):
    b, seq, d = x.shape
    sh1, sc1, gt1, sh2, sc2, gt2 = [m[:, None, :] for m in jnp.split(mod, 6, axis=-1)]
    x2 = x.reshape(b * seq, d)
    z = _inproj(x2, g_norm1, sc1, sh1, w_in, seq)
    x1 = _mixer(z.reshape(b, seq, -1), x, gt1, conv_w, conv_b, ln_g, ln_b, short_w, w_out)
    y = _peer(x1.reshape(b * seq, d), g_norm2, sc2, sh2, gt2, g_final, w_query, keys, eu, ev, seq)
    return y.reshape(b, seq, d)


def kernel(x_prompt, x_sample, c_prompt, c_sample, w_ada, b_ada, g_norm1, w_in, conv_w, conv_b,
           ln_g, ln_b, short_w, w_out, g_norm2, w_query, sub_keys, expert_u, expert_v, g_final):
    assert w_ada.shape[0] == 1, "one layer"
    bp, bs = c_prompt.shape[0], c_sample.shape[0]
    rows = -(-(bp + bs) // 8) * 8
    c_all = jnp.concatenate([c_prompt, c_sample, jnp.zeros((rows - bp - bs, c_prompt.shape[1]), F32)])
    mod = _adaln(c_all, w_ada[0], b_ada)
    n_hp = sub_keys.shape[1] * sub_keys.shape[2]
    weights = (g_norm1, w_in[0].astype(BF16), conv_w[0], conv_b, ln_g, ln_b, short_w[0],
               w_out[0].astype(BF16), g_norm2, w_query[0].astype(BF16),
               sub_keys[0].reshape(n_hp, sub_keys.shape[3], sub_keys.shape[4]).astype(BF16),
               expert_u[0].astype(BF16), expert_v[0].astype(BF16), g_final[None, :])
    y_prompt = _trunk(x_prompt, mod[:bp], *weights)
    y_sample = _trunk(x_sample, mod[bp:bp + bs], *weights)
    return (y_prompt, y_sample)
```

```python
import functools
import math

import jax
import jax.numpy as jnp
from jax import lax
from jax.experimental import pallas as pl
from jax.experimental.pallas import tpu as pltpu

F32 = jnp.float32
BF16 = jnp.bfloat16

RMS_EPS = 1e-6
LN_EPS = 1e-5
TOPK = 16
PEER_HEADS = 8
N_KEYS = 128
CONV_HALO = 16
LANES = 128
VMEM_LIMIT = 62 * 1024 * 1024
DMA_RING = 2
DMA_CHUNKS = 4

_NT = (((1,), (1,)), ((), ()))
_TN = (((0,), (0,)), ((), ()))


def _rmsnorm(x, g):
    ms = jnp.mean(x * x, axis=-1, keepdims=True)
    return x * lax.rsqrt(ms + RMS_EPS) * g


def _params(sem, flags=None):
    return pltpu.CompilerParams(dimension_semantics=sem, vmem_limit_bytes=VMEM_LIMIT, flags=flags)


def _const_spec(shape):
    nd = len(shape)
    return pl.BlockSpec(shape, lambda *_: (0,) * nd, pipeline_mode=pl.Buffered(1))


def _adaln_kernel(c_ref, w_ref, b_ref, o_ref):
    c = c_ref[...]
    o_ref[...] = jnp.dot(c * jax.nn.sigmoid(c), w_ref[...], preferred_element_type=F32) + b_ref[...]


def _adaln(c, w, b):
    r, d = c.shape
    n = w.shape[1]
    tn = next(c for c in (1024, 512, 256, LANES) if n % c == 0)
    return pl.pallas_call(
        _adaln_kernel,
        grid=(n // tn,),
        in_specs=[pl.BlockSpec((r, d), lambda j: (0, 0)),
                  pl.BlockSpec((d, tn), lambda j: (0, j)),
                  pl.BlockSpec((1, tn), lambda j: (0, j))],
        out_specs=pl.BlockSpec((r, tn), lambda j: (0, j)),
        out_shape=jax.ShapeDtypeStruct((r, n), F32),
        compiler_params=_params(("arbitrary",)),
        name="adaln",
    )(c, w, b)


def _inproj_kernel(x_ref, g_ref, sc_ref, sh_ref, w_ref, z_ref, h_ref):
    @pl.when(pl.program_id(1) == 0)
    def _():
        h = _rmsnorm(x_ref[...], g_ref[...]) * (1.0 + sc_ref[0]) + sh_ref[0]
        h_ref[...] = h.astype(BF16)

    z_ref[...] = jnp.dot(h_ref[...], w_ref[...], preferred_element_type=F32).astype(z_ref.dtype)


def _inproj(x2, g, sc, sh, w, seq):
    t, d = x2.shape
    n = w.shape[1]
    tm = min(1024, seq)
    tn = n // 5 if (n // 5) % LANES == 0 else n
    bpb = seq // tm
    return pl.pallas_call(
        _inproj_kernel,
        grid=(t // tm, n // tn),
        in_specs=[pl.BlockSpec((tm, d), lambda i, j: (i, 0)),
                  pl.BlockSpec((1, d), lambda i, j: (0, 0)),
                  pl.BlockSpec((1, 1, d), lambda i, j: (i // bpb, 0, 0)),
                  pl.BlockSpec((1, 1, d), lambda i, j: (i // bpb, 0, 0)),
                  pl.BlockSpec((d, tn), lambda i, j: (0, j))],
        out_specs=pl.BlockSpec((tm, tn), lambda i, j: (i, j)),
        out_shape=jax.ShapeDtypeStruct((t, n), BF16),
        scratch_shapes=[pltpu.VMEM((tm, d), BF16)],
        compiler_params=_params(("parallel", "arbitrary")),
        name="in_proj",
    )(x2, g, sc, sh, w)


def _mixer_kernel(z_ref, zp_ref, zn_ref, x_ref, gt_ref, cw_ref, cb_ref, lg_ref, lb_ref, sw_ref,
                  wo_ref, o_ref, ubuf, vbuf, cbuf, lhs, *, dc, dsh, conv_width, short_width):
    s = pl.program_id(1)
    ns = pl.num_programs(1)
    sb = x_ref.shape[1]
    halo = CONV_HALO
    rows = 64 if sb % 64 == 0 else sb

    def glu(zz):
        return zz[:, :dc] * jax.nn.sigmoid(zz[:, dc:2 * dc])

    def cx(zz):
        return zz[:, 2 * dc + dsh:2 * dc + 2 * dsh] * zz[:, 2 * dc + 2 * dsh:]

    zp = jnp.where(s > 0, zp_ref[0].astype(F32), 0.0)
    zn = jnp.where(s < ns - 1, zn_ref[0].astype(F32), 0.0)
    ubuf[0:halo, :] = glu(zp)
    ubuf[halo + sb:, :] = glu(zn)
    vbuf[0:halo, :] = cx(zp)
    vbuf[halo + sb:, :] = cx(zn)
    for r0 in range(0, sb, rows):
        zz = z_ref[0, r0:r0 + rows, :].astype(F32)
        ubuf[halo + r0:halo + r0 + rows, :] = glu(zz)
        vbuf[halo + r0:halo + r0 + rows, :] = cx(zz)

    cpad = (conv_width - 1) // 2

    base = halo - cpad
    assert 0 <= base and base + conv_width - 1 <= 2 * halo and rows % 8 == 0

    def conv_body(ct, carry):
        c0 = pl.multiple_of(ct * LANES, LANES)
        cs = pl.ds(c0, LANES)
        for r0 in range(0, sb, rows):
            acc = jnp.broadcast_to(cb_ref[:, cs], (rows, LANES))
            for s in range(8):
                part = None
                for k in range(conv_width):
                    if (base + k) % 8 != s:
                        continue
                    a8 = base + k - s
                    term = cw_ref[k:k + 1, cs] * ubuf[r0 + a8:r0 + a8 + rows + 8, cs]
                    part = term if part is None else part + term
                if part is not None:
                    acc = acc + part[s:s + rows, :]
            cbuf[r0:r0 + rows, cs] = acc
        return carry

    lax.fori_loop(0, dc // LANES, conv_body, 0)

    spad = (short_width - 1) // 2
    for r0 in range(0, sb, rows):
        y = cbuf[r0:r0 + rows, :]
        mu = jnp.mean(y, axis=-1, keepdims=True)
        yc = y - mu
        var = jnp.mean(yc * yc, axis=-1, keepdims=True)
        yn = yc * lax.rsqrt(var + LN_EPS) * lg_ref[...] + lb_ref[...]
        lhs[r0:r0 + rows, 0:dc] = (yn * jax.nn.sigmoid(yn)).astype(BF16)
        sacc = jnp.zeros((rows, dsh), F32)
        for k in range(short_width):
            start = halo + r0 - spad + k
            sacc = sacc + sw_ref[k:k + 1, :] * vbuf[start:start + rows, :]
        bg = z_ref[0, r0:r0 + rows, 2 * dc:2 * dc + dsh].astype(F32)
        lhs[r0:r0 + rows, dc:dc + dsh] = (bg * sacc).astype(BF16)

    mix = jnp.dot(lhs[...], wo_ref[...], preferred_element_type=F32)
    o_ref[0] = x_ref[0] + gt_ref[0] * mix


def _mixer(z3, x3, gt, conv_w, conv_b, ln_g, ln_b, short_w, w_out):
    b, seq, d = x3.shape
    d_in = z3.shape[-1]
    dc = conv_w.shape[1]
    dsh = short_w.shape[1]
    sb = min(256, seq)
    halo = CONV_HALO
    hpb = sb // halo
    n_halo = seq // halo
    kern = functools.partial(_mixer_kernel, dc=dc, dsh=dsh, conv_width=conv_w.shape[0],
                             short_width=short_w.shape[0])
    return pl.pallas_call(
        kern,
        grid=(b, seq // sb),
        in_specs=[pl.BlockSpec((1, sb, d_in), lambda i, j: (i, j, 0)),
                  pl.BlockSpec((1, halo, d_in), lambda i, j: (i, jnp.maximum(j * hpb - 1, 0), 0)),
                  pl.BlockSpec((1, halo, d_in),
                               lambda i, j: (i, jnp.minimum((j + 1) * hpb, n_halo - 1), 0)),
                  pl.BlockSpec((1, sb, d), lambda i, j: (i, j, 0)),
                  pl.BlockSpec((1, 1, d), lambda i, j: (i, 0, 0)),
                  _const_spec(conv_w.shape), _const_spec(conv_b.shape), _const_spec(ln_g.shape),
                  _const_spec(ln_b.shape), _const_spec(short_w.shape), _const_spec(w_out.shape)],
        out_specs=pl.BlockSpec((1, sb, d), lambda i, j: (i, j, 0)),
        out_shape=jax.ShapeDtypeStruct((b, seq, d), F32),
        scratch_shapes=[pltpu.VMEM((sb + 2 * halo, dc), F32),
                        pltpu.VMEM((sb + 2 * halo, dsh), F32),
                        pltpu.VMEM((sb, dc), F32),
                        pltpu.VMEM((sb, dc + dsh), BF16)],
        compiler_params=_params(("parallel", "arbitrary")),
        name="mixer",
    )(z3, z3, z3, x3, gt, conv_w, conv_b, ln_g, ln_b, short_w, w_out)


def _staircase(k):
    return [(a, b) for a in range(k) for b in range(k) if (a + 1) * (b + 1) <= k]


def _sorting_network(n):
    pairs = []
    p = 1
    while p < n:
        k = p
        while k >= 1:
            for j in range(k % p, n - k, 2 * k):
                for i in range(min(k, n - j - k)):
                    if (i + j) // (2 * p) == (i + j + k) // (2 * p):
                        pairs.append((i + j, i + j + k))
            k //= 2
        p *= 2
    return pairs


def _peer_kernel(x_ref, g2_ref, sc_ref, sh_ref, gt_ref, gf_ref, wq_hbm, keys_ref, eu_hbm, ev_hbm,
                 o_ref, hb_ref, s_ref, vals_ref, stat_ref, th_ref, act0_ref, act1_ref, coef0_ref,
                 coef1_ref, ubuf, vbuf, qbuf, usem, vsem, qsem, *, heads, n_keys, topk):
    tb = x_ref.shape[0]
    eb = ubuf.shape[1]
    nb = eu_hbm.shape[0] // eb
    dh = keys_ref.shape[2]
    n_lt = tb // LANES
    ni = eb // n_keys
    kk = topk + 1
    assert nb % 2 == 0 and 2 * ni == 8, "two expert blocks share one 8-row tile of per-key rows"

    ring = ubuf.shape[0]
    ahead = ring - 1
    sub = eb // DMA_CHUNKS

    def block_copies(hbm, buf, sem, blk, rslot):
        return [pltpu.make_async_copy(hbm.at[pl.ds(blk * eb + c * sub, sub), :],
                                      buf.at[rslot, pl.ds(c * sub, sub), :], sem.at[rslot])
                for c in range(DMA_CHUNKS)]

    def start_block(hbm, buf, sem, blk):
        rslot = blk % ring if isinstance(blk, int) else lax.rem(blk, ring)
        for c, cp in enumerate(block_copies(hbm, buf, sem, blk, rslot)):
            cp.start(priority=c % 2)

    def wait_block(hbm, buf, sem, blk):
        rslot = blk % ring if isinstance(blk, int) else lax.rem(blk, ring)
        for cp in block_copies(hbm, buf, sem, blk, rslot):
            cp.wait()
        return rslot

    def q_copy(c):
        return pltpu.make_async_copy(wq_hbm.at[c], qbuf.at[c % 2], qsem.at[c % 2])

    q_copy(0).start()
    for b in range(ahead):
        start_block(eu_hbm, ubuf, usem, b)

    def _prologue():
        h = _rmsnorm(x_ref[...], g2_ref[...]) * (1.0 + sc_ref[0]) + sh_ref[0]
        hb_ref[...] = h.astype(BF16)
        n_qc, _, qw = wq_hbm.shape
        tq = min(tb, 4 * LANES)
        for c in range(n_qc):
            if c + 1 < n_qc:
                q_copy(c + 1).start()
            q_copy(c).wait()
            for t0 in range(0, tb, tq):
                q = jnp.dot(hb_ref[t0:t0 + tq, :], qbuf[c % 2],
                            preferred_element_type=F32).astype(BF16)
                for r in range(qw // dh):
                    hp = c * (qw // dh) + r
                    s = lax.dot_general(keys_ref[hp], q[:, r * dh:(r + 1) * dh], _NT,
                                        preferred_element_type=F32)
                    for l in range(tq // LANES):
                        s_ref[hp, t0 // LANES + l] = s[:, l * LANES:(l + 1) * LANES]

        n_tiles = n_keys // 8
        network = _sorting_network(n_tiles)

        def top_body(lt, carry):
            ls = pl.ds(pl.multiple_of(lt * LANES, LANES), LANES)
            for hp in range(2 * heads):
                col = [s_ref[hp, lt, 8 * k:8 * (k + 1), :] for k in range(n_tiles)]
                for a, b in network:
                    col[a], col[b] = jnp.maximum(col[a], col[b]), jnp.minimum(col[a], col[b])
                for r in range(kk):
                    m = jnp.max(col[0], axis=0, keepdims=True)
                    vals_ref[hp % 2, r, hp // 2:hp // 2 + 1, ls] = m
                    pop = col[0] == m
                    for k in range(min(kk - 1 - r, n_tiles)):
                        below = col[k + 1] if k + 1 < n_tiles else -jnp.inf
                        col[k] = jnp.where(pop, below, col[k])
            return carry

        lax.fori_loop(0, n_lt, top_body, 0)

        pairs = _staircase(kk)

        def cand_body(lt, carry):
            ls = pl.ds(pl.multiple_of(lt * LANES, LANES), LANES)
            v0 = [vals_ref[0, a, :, ls] for a in range(kk)]
            v1 = [vals_ref[1, a, :, ls] for a in range(kk)]
            cands = [v0[a] + v1[b] for a, b in pairs]
            best = []
            for r in range(kk):
                m = functools.reduce(jnp.maximum, cands)
                best.append(m)
                if r + 1 < kk:
                    cands = [jnp.where(c == m, -jnp.inf, c) for c in cands]
            z = functools.reduce(lambda p, c: p + c, [jnp.exp(c - best[0]) for c in best[:topk]])
            stat_ref[0, :, ls] = 0.5 * (best[topk - 1] + best[topk])
            stat_ref[1, :, ls] = 1.0 / z
            return carry

        lax.fori_loop(0, n_lt, cand_body, 0, unroll=2)

        def state_body(lt, carry):
            ls = pl.ds(pl.multiple_of(lt * LANES, LANES), LANES)
            for hd in range(heads):
                s0 = s_ref[2 * hd, lt]
                s1 = s_ref[2 * hd + 1, lt]
                m1 = vals_ref[1, 0, hd:hd + 1, ls]
                th_ref[hd, lt] = jnp.exp(stat_ref[0, hd:hd + 1, ls] - s0 - m1)
                s_ref[2 * hd, lt] = jnp.exp(s0 - vals_ref[0, 0, hd:hd + 1, ls]) * stat_ref[1, hd:hd + 1, ls]
                s_ref[2 * hd + 1, lt] = jnp.exp(s1 - m1)
            return carry

        lax.fori_loop(0, n_lt, state_body, 0)

        o_ref[...] = jnp.zeros(o_ref.shape, F32)

    _prologue()

    acts = (act0_ref, act1_ref)
    coefs = (coef0_ref, coef1_ref)

    def scores(par, rslot):
        acts[par][...] = lax.dot_general(ubuf[rslot], hb_ref[...], _NT, preferred_element_type=F32)

    def gates(par, blk, lane_tiles):
        row0 = (blk // 2) * 8 if isinstance(blk, int) else pl.multiple_of(lax.div(blk, 2) * 8, 8)
        off = par * ni
        for lt in lane_tiles:
            ls = slice(lt * LANES, (lt + 1) * LANES)
            thr = [th_ref[hd, lt, pl.ds(row0, 8), :] for hd in range(heads)]
            wr = [s_ref[2 * hd, lt, pl.ds(row0, 8), :] for hd in range(heads)]
            for ii in range(ni):
                gate = jnp.zeros((n_keys, LANES), F32)
                for hd in range(heads):
                    g1 = s_ref[2 * hd + 1, lt]
                    sel = g1 >= thr[hd][off + ii:off + ii + 1, :]
                    gate = gate + jnp.where(sel, g1 * wr[hd][off + ii:off + ii + 1, :], 0.0)
                a = acts[par][ii * n_keys:(ii + 1) * n_keys, ls]
                gelu = 0.5 * a * (1.0 + lax.erf(a * (1.0 / math.sqrt(2.0))))
                coefs[par][ii * n_keys:(ii + 1) * n_keys, ls] = (gelu * gate).astype(BF16)

    def outputs(par, rslot):
        o_ref[...] += lax.dot_general(coefs[par][...], vbuf[rslot], _TN, preferred_element_type=F32)

    def slot(k, par):
        static = isinstance(k, int)
        do_scores = (k < nb) if static else True
        do_gates = (1 <= k <= nb) if static else True
        do_outputs = (k >= 2) if static else True
        if static:
            if k + ahead < nb:
                start_block(eu_hbm, ubuf, usem, k + ahead)
            if 0 <= k - 2 + ahead < nb:
                start_block(ev_hbm, vbuf, vsem, k - 2 + ahead)
        else:
            @pl.when(k + ahead < nb)
            def _():
                start_block(eu_hbm, ubuf, usem, k + ahead)

            @pl.when(k - 2 + ahead < nb)
            def _():
                start_block(ev_hbm, vbuf, vsem, k - 2 + ahead)
        if do_scores:
            u_slot = wait_block(eu_hbm, ubuf, usem, k)
        if do_outputs:
            v_slot = wait_block(ev_hbm, vbuf, vsem, k - 2)
        if do_scores:
            scores(par, u_slot)
        if do_gates:
            gates(1 - par, k - 1, range(0, n_lt // 2))
        if do_outputs:
            outputs(par, v_slot)
        if do_gates:
            gates(1 - par, k - 1, range(n_lt // 2, n_lt))

    slot(0, 0)
    slot(1, 1)

    def steady(m, carry):
        slot(2 * m, 0)
        slot(2 * m + 1, 1)
        return carry

    lax.fori_loop(1, nb // 2, steady, 0)
    slot(nb, 0)
    slot(nb + 1, 1)

    y = x_ref[...] + gt_ref[0] * o_ref[...]
    o_ref[...] = _rmsnorm(y, gf_ref[...])


def _peer(x2, g2, sc, sh, gt, gf, wq, keys, eu, ev, seq):
    t, d = x2.shape
    n_exp = eu.shape[0]
    n_hp, n_keys, dh = keys.shape
    heads = n_hp // 2
    tb = min(1024, seq)
    eb = 512
    qw = 4 * dh
    bpb = seq // tb
    n_lt = tb // LANES
    assert n_exp % (2 * eb) == 0 and eu.shape == ev.shape == (n_exp, d) and wq.shape[1] % qw == 0
    wq_chunks = wq.reshape(d, wq.shape[1] // qw, qw).transpose(1, 0, 2)
    kern = functools.partial(_peer_kernel, heads=heads, n_keys=n_keys, topk=TOPK)
    mod_spec = pl.BlockSpec((1, 1, d), lambda i: (i // bpb, 0, 0))
    hbm_spec = pl.BlockSpec(memory_space=pl.ANY)
    return pl.pallas_call(
        kern,
        grid=(t // tb,),
        in_specs=[pl.BlockSpec((tb, d), lambda i: (i, 0), pipeline_mode=pl.Buffered(1)),
                  _const_spec(g2.shape), mod_spec, mod_spec, mod_spec, _const_spec(gf.shape),
                  hbm_spec, _const_spec(keys.shape), hbm_spec, hbm_spec],
        out_specs=pl.BlockSpec((tb, d), lambda i: (i, 0), pipeline_mode=pl.Buffered(1)),
        out_shape=jax.ShapeDtypeStruct((t, d), F32),
        scratch_shapes=[pltpu.VMEM((tb, d), BF16),
                        pltpu.VMEM((n_hp, n_lt, n_keys, LANES), F32),
                        pltpu.VMEM((2, TOPK + 1, heads, tb), F32),
                        pltpu.VMEM((2, heads, tb), F32),
                        pltpu.VMEM((heads, n_lt, n_keys, LANES), F32),
                        pltpu.VMEM((eb, tb), F32), pltpu.VMEM((eb, tb), F32),
                        pltpu.VMEM((eb, tb), BF16), pltpu.VMEM((eb, tb), BF16),
                        pltpu.VMEM((DMA_RING, eb, d), BF16),
                        pltpu.VMEM((DMA_RING, eb, d), BF16),
                        pltpu.VMEM((2, d, qw), BF16),
                        pltpu.SemaphoreType.DMA((DMA_RING,)), pltpu.SemaphoreType.DMA((DMA_RING,)),
                        pltpu.SemaphoreType.DMA((2,))],
        compiler_params=_params(("parallel",)),
        name="peer",
    )(x2, g2, sc, sh, gt, gf, wq_chunks, keys, eu, ev)


def _trunk(x, mod, g_norm1, w_in, conv_w, conv_b, ln_g, ln_b, short_w, w_out, g_norm2, w_query,
           keys, eu, ev, g_final):
    b, seq, d = x.shape
    sh1, sc1, gt1, sh2, sc2, gt2 = [m[:, None, :] for m in jnp.split(mod, 6, axis=-1)]
    x2 = x.reshape(b * seq, d)
    z = _inproj(x2, g_norm1, sc1, sh1, w_in, seq)
    x1 = _mixer(z.reshape(b, seq, -1), x, gt1, conv_w, conv_b, ln_g, ln_b, short_w, w_out)
    y = _peer(x1.reshape(b * seq, d), g_norm2, sc2, sh2, gt2, g_final, w_query, keys, eu, ev, seq)
    return y.reshape(b, seq, d)


def kernel(x_prompt, x_sample, c_prompt, c_sample, w_ada, b_ada, g_norm1, w_in, conv_w, conv_b,
           ln_g, ln_b, short_w, w_out, g_norm2, w_query, sub_keys, expert_u, expert_v, g_final):
    assert w_ada.shape[0] == 1, "one layer"
    bp, bs = c_prompt.shape[0], c_sample.shape[0]
    rows = -(-(bp + bs) // 8) * 8
    c_all = jnp.concatenate([c_prompt, c_sample, jnp.zeros((rows - bp - bs, c_prompt.shape[1]), F32)])
    mod = _adaln(c_all, w_ada[0], b_ada)
    n_hp = sub_keys.shape[1] * sub_keys.shape[2]
    weights = (g_norm1, w_in[0].astype(BF16), conv_w[0], conv_b, ln_g, ln_b, short_w[0],
               w_out[0].astype(BF16), g_norm2, w_query[0].astype(BF16),
               sub_keys[0].reshape(n_hp, sub_keys.shape[3], sub_keys.shape[4]).astype(BF16),
               expert_u[0].astype(BF16), expert_v[0].astype(BF16), g_final[None, :])
    y_prompt = _trunk(x_prompt, mod[:bp], *weights)
    y_sample = _trunk(x_sample, mod[bp:bp + bs], *weights)
    return (y_prompt, y_sample)
```
